```python
import math
import jax, jax.numpy as jnp
from jax import lax
import numpy as np

D_MODEL = 1024
BATCH = 8
SEQ = 4096
DEPTH = 2

HEAD_DIM = 64
A_HEADS = 6
A_KV_HEADS = 2
A_WINDOW = 128
B_HEADS = 4
NSA_CMP_LEN = 32
NSA_CMP_STRIDE = 16
NSA_SLC_LEN = 64
NSA_SLC_TOPK = 16
NSA_WINDOW = 512
C_HEADS = 6
N_HEADS = A_HEADS + B_HEADS + C_HEADS
MIX_WIDTH = N_HEADS * HEAD_DIM
BLOCK = 128
REL_BUCKETS = 32
REL_MAX_DIST = 128
N_EXPERTS = 256
TOP_K = 8
N_GROUPS = 8
TOPK_GROUPS = 4
EXPERT_DIM = 256
SHARED_DIM = 256
ROUTED_SCALE = 2.5
MOE_ROW_BLOCK = 128
LN_EPS = 1e-5
ADA_INIT = 0.1
NEG = -1e30
FORCE_SCORE = 1e4

SPLIT_SIZES = (
    A_HEADS * HEAD_DIM, A_KV_HEADS * HEAD_DIM, A_KV_HEADS * HEAD_DIM,
    B_HEADS * HEAD_DIM, HEAD_DIM, HEAD_DIM, HEAD_DIM, HEAD_DIM, HEAD_DIM, HEAD_DIM,
    3 * B_HEADS,
    C_HEADS * HEAD_DIM, C_HEADS * HEAD_DIM, C_HEADS * HEAD_DIM, C_HEADS,
)
D_IN = sum(SPLIT_SIZES)

kernel_name = "hymba_style_swa_nsa_fox_moe_block"


def layer_norm(x, g, b):
    xf = x.astype(jnp.float32)
    mu = jnp.mean(xf, -1, keepdims=True)
    var = jnp.mean(jnp.square(xf - mu), -1, keepdims=True)
    return ((xf - mu) * lax.rsqrt(var + LN_EPS)).astype(x.dtype) * g + b


def t5_bucket(rel):
    n = jnp.maximum(rel, 0)
    max_exact = REL_BUCKETS // 2
    nf = jnp.maximum(n, 1).astype(jnp.float32)
    large = max_exact + (jnp.log(nf / max_exact) / math.log(REL_MAX_DIST / max_exact)
                         * (REL_BUCKETS - max_exact)).astype(jnp.int32)
    large = jnp.minimum(large, REL_BUCKETS - 1)
    return jnp.where(n < max_exact, n, large)


def banded_attention(q, k, v, window, table, sinks):
    bsz, t, h, d = q.shape
    kvh = k.shape[2]
    g = h // kvh
    nb = t // BLOCK
    n_prev = -(-(window - 1) // BLOCK)
    w = (n_prev + 1) * BLOCK
    qb = q.reshape(bsz, nb, BLOCK, kvh, g, d)

    def windows(a):
        ab = jnp.pad(a.reshape(bsz, nb, BLOCK, kvh, d), ((0, 0), (n_prev, 0), (0, 0), (0, 0), (0, 0)))
        return jnp.concatenate([ab[:, j:j + nb] for j in range(n_prev + 1)], axis=2)

    kw, vw = windows(k), windows(v)
    s = jnp.einsum('bnqkgd,bnskd->bnkgqs', qb, kw, preferred_element_type=jnp.float32) * HEAD_DIM ** -0.5
    rel = n_prev * BLOCK + jnp.arange(BLOCK)[:, None] - jnp.arange(w)[None, :]
    bias = jnp.transpose(table[t5_bucket(rel)], (2, 0, 1)).reshape(kvh, g, BLOCK, w).astype(jnp.float32)
    kpos = (jnp.arange(nb)[:, None] - n_prev) * BLOCK + jnp.arange(w)[None, :]
    mask = ((rel >= 0) & (rel < window))[None] & (kpos >= 0)[:, None, :]
    s = jnp.where(mask[None, :, None, None], s + bias, NEG)
    if sinks is None:
        p = jax.nn.softmax(s, axis=-1)
    else:
        sk = sinks.astype(jnp.float32).reshape(kvh, g)[None, None, :, :, None, None]
        m = jnp.maximum(jnp.max(s, -1, keepdims=True), sk)
        e = jnp.exp(s - m)
        p = e / (jnp.sum(e, -1, keepdims=True) + jnp.exp(sk - m))
    o = jnp.einsum('bnkgqs,bnskd->bnqkgd', p.astype(v.dtype), vw)
    return o.reshape(bsz, t, h, d)


def nsa_compress(x, pos, w):
    t = x.shape[1]
    n_cmp = (t - NSA_CMP_LEN) // NSA_CMP_STRIDE + 1
    idx = jnp.arange(n_cmp)[:, None] * NSA_CMP_STRIDE + jnp.arange(NSA_CMP_LEN)[None, :]
    blocks = x[:, idx] + pos
    return blocks.reshape(x.shape[0], n_cmp, -1) @ w


def selected_block_attention(q, k, v, sel, table):
    bsz, t, h, d = q.shape
    n_sel = sel.shape[-1]
    nq = t // BLOCK
    kb = k.reshape(bsz, t // NSA_SLC_LEN, NSA_SLC_LEN, d)
    vb = v.reshape(bsz, t // NSA_SLC_LEN, NSA_SLC_LEN, d)
    qs = q.reshape(bsz, nq, BLOCK, h, d).transpose(1, 0, 2, 3, 4)
    ss = sel.reshape(bsz, nq, BLOCK, n_sel).transpose(1, 0, 2, 3)
    take = jax.vmap(lambda blocks, ids: blocks[ids])

    def step(args):
        qb, sb, i = args
        kg = take(kb, sb).reshape(bsz, BLOCK, n_sel * NSA_SLC_LEN, d)
        vg = take(vb, sb).reshape(bsz, BLOCK, n_sel * NSA_SLC_LEN, d)
        tpos = i * BLOCK + jnp.arange(BLOCK)
        spos = (sb[..., None] * NSA_SLC_LEN + jnp.arange(NSA_SLC_LEN)).reshape(bsz, BLOCK, n_sel * NSA_SLC_LEN)
        rel = tpos[None, :, None] - spos
        bias = jnp.transpose(table[t5_bucket(rel)], (0, 3, 1, 2)).astype(jnp.float32)
        s = jnp.einsum('bqhd,bqkd->bhqk', qb, kg, preferred_element_type=jnp.float32) * HEAD_DIM ** -0.5
        s = jnp.where((rel >= 0)[:, None], s + bias, NEG)
        p = jax.nn.softmax(s, axis=-1)
        return jnp.einsum('bhqk,bqkd->bqhd', p.astype(v.dtype), vg)

    o = lax.map(step, (qs, ss, jnp.arange(nq)))
    return o.transpose(1, 0, 2, 3, 4).reshape(bsz, t, h, d)


def native_sparse_attention(q, kc, vc, ks, vs, kw, vw, gate_logits, pos_k, w_k, pos_v, w_v, table):
    bsz, t, h, d = q.shape
    k_cmp = nsa_compress(kc, pos_k, w_k)
    v_cmp = nsa_compress(vc, pos_v, w_v)
    n_cmp = k_cmp.shape[1]
    tpos = jnp.arange(t)
    c_start = jnp.arange(n_cmp) * NSA_CMP_STRIDE
    rel = tpos[:, None] - (c_start + NSA_CMP_LEN - 1)[None, :]
    valid = rel >= 0
    s = jnp.einsum('bthd,bnd->bhtn', q, k_cmp, preferred_element_type=jnp.float32) * HEAD_DIM ** -0.5
    s = jnp.where(valid, s + jnp.transpose(table[t5_bucket(rel)], (2, 0, 1)).astype(jnp.float32), NEG)
    p_cmp = jax.nn.softmax(s, axis=-1) * jnp.any(valid, -1)[:, None]
    o_cmp = jnp.einsum('bhtn,bnd->bthd', p_cmp.astype(vc.dtype), v_cmp)
    n_slc = t // NSA_SLC_LEN
    s_start = jnp.arange(n_slc) * NSA_SLC_LEN
    overlap = jnp.clip(jnp.minimum(c_start[:, None] + NSA_CMP_LEN, s_start[None, :] + NSA_SLC_LEN)
                       - jnp.maximum(c_start[:, None], s_start[None, :]), 0).astype(jnp.float32) / NSA_CMP_LEN
    imp = jnp.einsum('bhtn,ns->bts', p_cmp, overlap)
    cur = (tpos // NSA_SLC_LEN)[:, None]
    j = jnp.arange(n_slc)[None, :]
    forced = (j == 0) | (j == cur) | (j == cur - 1)
    imp = jnp.where(forced, FORCE_SCORE, jnp.where(j > cur, -1.0, imp))
    _, sel = lax.top_k(imp, min(NSA_SLC_TOPK, n_slc))
    o_slc = selected_block_attention(q, ks, vs, sel, table)
    o_win = banded_attention(q, kw[:, :, None], vw[:, :, None], NSA_WINDOW, table, None)
    g = jax.nn.sigmoid(gate_logits.astype(jnp.float32)).reshape(bsz, t, 3, h)[..., None].astype(q.dtype)
    return g[:, :, 0] * o_cmp + g[:, :, 1] * o_slc + g[:, :, 2] * o_win


def forgetting_attention(q, k, v, log_f):
    bsz, t, h, d = q.shape
    nq = t // BLOCK
    dcum = jnp.cumsum(log_f.astype(jnp.float32), axis=1)
    d_k = jnp.transpose(dcum, (0, 2, 1))
    qs = q.reshape(bsz, nq, BLOCK, h, d).transpose(1, 0, 2, 3, 4)
    ds = dcum.reshape(bsz, nq, BLOCK, h).transpose(1, 0, 3, 2)
    spos = jnp.arange(t)

    def step(args):
        qb, dq, i = args
        tpos = i * BLOCK + jnp.arange(BLOCK)
        s = (jnp.einsum('bqhd,bshd->bhqs', qb, k, preferred_element_type=jnp.float32) * HEAD_DIM ** -0.5
             + dq[..., None] - d_k[:, :, None, :])
        s = jnp.where(spos[None, :] <= tpos[:, None], s, NEG)
        p = jax.nn.softmax(s, axis=-1)
        return jnp.einsum('bhqs,bshd->bqhd', p.astype(v.dtype), v)

    o = lax.map(step, (qs, ds, jnp.arange(nq)))
    return o.transpose(1, 0, 2, 3, 4).reshape(bsz, t, h, d)


def hybrid_mixer(h, w_in, w_o, sinks, pos_k, w_k, pos_v, w_v, f_bias, table):
    bsz, t, _ = h.shape
    cuts = np.cumsum(SPLIT_SIZES)[:-1].tolist()
    (aq, ak, av, bq, bkc, bvc, bks, bvs, bkw, bvw, bg, cq, ck, cv, cf) = jnp.split(h @ w_in, cuts, axis=-1)
    heads = lambda a, n: a.reshape(bsz, t, n, HEAD_DIM)
    o_a = banded_attention(heads(aq, A_HEADS), heads(ak, A_KV_HEADS), heads(av, A_KV_HEADS),
                           A_WINDOW, table[:, :A_HEADS], sinks)
    o_b = native_sparse_attention(heads(bq, B_HEADS), bkc, bvc, bks, bvs, bkw, bvw, bg,
                                  pos_k, w_k, pos_v, w_v, table[:, A_HEADS:])
    o_c = forgetting_attention(heads(cq, C_HEADS), heads(ck, C_HEADS), heads(cv, C_HEADS),
                               jax.nn.log_sigmoid((cf + f_bias).astype(jnp.float32)))
    o = jnp.concatenate([o_a, o_b, o_c], axis=2).reshape(bsz, t, MIX_WIDTH)
    return o @ w_o


def swiglu(x, wg, wu, wd):
    return (jax.nn.silu(x @ wg) * (x @ wu)) @ wd


def route(h, w_router, router_bias):
    scores = jax.nn.sigmoid((h @ w_router).astype(jnp.float32))
    choice = scores + router_bias.astype(jnp.float32)
    grouped = choice.reshape(-1, N_GROUPS, N_EXPERTS // N_GROUPS)
    gscore = jnp.sum(lax.top_k(grouped, 2)[0], axis=-1)
    _, gidx = lax.top_k(gscore, TOPK_GROUPS)
    gmask = jnp.sum(jax.nn.one_hot(gidx, N_GROUPS, dtype=jnp.float32), axis=-2) > 0
    masked = jnp.where(gmask[:, :, None], grouped, NEG).reshape(-1, N_EXPERTS)
    _, idx = lax.top_k(masked, TOP_K)
    w = jnp.take_along_axis(scores, idx, axis=-1)
    w = w / jnp.sum(w, -1, keepdims=True) * ROUTED_SCALE
    return idx, w


def moe_routed(h, idx, gw, w_gate, w_up, w_down):
    n, d = h.shape
    a = n * TOP_K
    flat_e = idx.reshape(-1)
    flat_tok = jnp.arange(a) // TOP_K
    flat_g = gw.reshape(-1)
    order = jnp.argsort(flat_e)
    se, stok, sg = flat_e[order], flat_tok[order], flat_g[order]
    counts = jnp.bincount(flat_e, length=N_EXPERTS)
    starts = jnp.cumsum(counts) - counts
    padded = (counts + MOE_ROW_BLOCK - 1) // MOE_ROW_BLOCK * MOE_ROW_BLOCK
    pends = jnp.cumsum(padded)
    pstarts = pends - padded
    dest = pstarts[se] + (jnp.arange(a) - starts[se])
    n_blocks = -(-a // MOE_ROW_BLOCK) + N_EXPERTS
    rows = n_blocks * MOE_ROW_BLOCK
    row_tok = jnp.full((rows,), n, jnp.int32).at[dest].set(stok)
    row_g = jnp.zeros((rows,), h.dtype).at[dest].set(sg)
    block_e = jnp.clip(jnp.searchsorted(pends, jnp.arange(n_blocks) * MOE_ROW_BLOCK, side='right'),
                       0, N_EXPERTS - 1)
    h_ext = jnp.concatenate([h, jnp.zeros((1, d), h.dtype)], axis=0)

    def step(out, xs):
        tok, g, e = xs
        y = swiglu(h_ext[tok], w_gate[e], w_up[e], w_down[e])
        return out.at[tok].add(y * g[:, None]), None

    out, _ = lax.scan(step, jnp.zeros((n + 1, d), h.dtype),
                      (row_tok.reshape(n_blocks, MOE_ROW_BLOCK), row_g.reshape(n_blocks, MOE_ROW_BLOCK), block_e))
    return out[:n]


def setup_inputs(seed: int = 0) -> dict:
    key = jax.random.key(seed)
    ks = jax.random.split(key, 26)
    nrm = lambda k, shape: jax.random.normal(k, shape, jnp.float32)
    beta = (8.0 * DEPTH) ** -0.25
    D, L, Dh, E, F, Fs = D_MODEL, NSA_CMP_LEN, HEAD_DIM, N_EXPERTS, EXPERT_DIM, SHARED_DIM
    return {
        'x': nrm(ks[0], (BATCH, SEQ, D)),
        'c': nrm(ks[1], (BATCH, D)),
        'w_in': nrm(ks[2], (DEPTH, D, D_IN)) * D ** -0.5,
        'w_o': nrm(ks[3], (DEPTH, MIX_WIDTH, D)) * MIX_WIDTH ** -0.5 * beta,
        'a_sinks': nrm(ks[4], (DEPTH, A_HEADS)),
        'nsa_pos_k': 0.02 * nrm(ks[5], (DEPTH, L, Dh)),
        'nsa_w_k': nrm(ks[6], (DEPTH, L * Dh, Dh)) * (L * Dh) ** -0.5,
        'nsa_pos_v': 0.02 * nrm(ks[7], (DEPTH, L, Dh)),
        'nsa_w_v': nrm(ks[8], (DEPTH, L * Dh, Dh)) * (L * Dh) ** -0.5,
        'fox_f_bias': 3.0 + 0.5 * nrm(ks[9], (DEPTH, C_HEADS)),
        'rel_bias': 0.5 * nrm(ks[10], (REL_BUCKETS, A_HEADS + B_HEADS)),
        'w_ada': nrm(ks[11], (DEPTH, D, 6 * D)) * ADA_INIT * D ** -0.5,
        'b_ada': 0.01 * nrm(ks[12], (DEPTH, 6 * D)),
        'ln1_g': 1.0 + 0.01 * nrm(ks[13], (DEPTH, D)),
        'ln1_b': 0.01 * nrm(ks[14], (DEPTH, D)),
        'ln2_g': 1.0 + 0.01 * nrm(ks[15], (DEPTH, D)),
        'ln2_b': 0.01 * nrm(ks[16], (DEPTH, D)),
        'w_router': nrm(ks[17], (DEPTH, D, E)) * D ** -0.5,
        'router_bias': 0.01 * nrm(ks[18], (DEPTH, E)),
        'w_exp_gate': nrm(ks[19], (DEPTH, E, D, F)) * D ** -0.5,
        'w_exp_up': nrm(ks[20], (DEPTH, E, D, F)) * D ** -0.5,
        'w_exp_down': nrm(ks[21], (DEPTH, E, F, D)) * F ** -0.5 * beta,
        'w_sh_gate': nrm(ks[22], (DEPTH, D, Fs)) * D ** -0.5,
        'w_sh_up': nrm(ks[23], (DEPTH, D, Fs)) * D ** -0.5,
        'w_sh_down': nrm(ks[24], (DEPTH, Fs, D)) * Fs ** -0.5 * beta,
    }


def reference(x, c, w_in, w_o, a_sinks, nsa_pos_k, nsa_w_k, nsa_pos_v, nsa_w_v, fox_f_bias, rel_bias,
              w_ada, b_ada, ln1_g, ln1_b, ln2_g, ln2_b, w_router, router_bias,
              w_exp_gate, w_exp_up, w_exp_down, w_sh_gate, w_sh_up, w_sh_down):
    alpha = (2.0 * DEPTH) ** 0.25
    bsz, t, d = x.shape
    cond = jax.nn.silu(c)
    for l in range(DEPTH):
        ada = cond @ w_ada[l] + b_ada[l]
        sh1, sc1, g1, sh2, sc2, g2 = [a[:, None, :] for a in jnp.split(ada, 6, axis=-1)]
        h = x * (1 + sc1) + sh1
        y = hybrid_mixer(h, w_in[l], w_o[l], a_sinks[l], nsa_pos_k[l], nsa_w_k[l], nsa_pos_v[l], nsa_w_v[l],
                         fox_f_bias[l], rel_bias)
        x = layer_norm(alpha * x + (1 + g1) * y, ln1_g[l], ln1_b[l])
        h = (x * (1 + sc2) + sh2).reshape(bsz * t, d)
        idx, gw = route(h, w_router[l], router_bias[l])
        y = (swiglu(h, w_sh_gate[l], w_sh_up[l], w_sh_down[l])
             + moe_routed(h, idx, gw.astype(h.dtype), w_exp_gate[l], w_exp_up[l], w_exp_down[l]))
        x = layer_norm(alpha * x + (1 + g2) * y.reshape(bsz, t, d), ln2_g[l], ln2_b[l])
    return x
```

```python
import functools
import math

import numpy as np
import jax
import jax.numpy as jnp
from jax import lax
from jax.experimental import pallas as pl
from jax.experimental.pallas import tpu as pltpu

HEAD_DIM = 64
A_HEADS = 6
A_KV_HEADS = 2
A_WINDOW = 128
B_HEADS = 4
NSA_CMP_LEN = 32
NSA_CMP_STRIDE = 16
NSA_SLC_LEN = 64
NSA_SLC_TOPK = 16
NSA_WINDOW = 512
C_HEADS = 6
BLOCK = 128
REL_BUCKETS = 32
REL_MAX_DIST = 128
N_EXPERTS = 256
TOP_K = 8
N_GROUPS = 8
TOPK_GROUPS = 4
ROUTED_SCALE = 2.5
LN_EPS = 1e-5
NEG = -1e30
FORCE_SCORE = 1e4

LANES = 128
PROJ_ROWS = 512
C_TQ = 256
C_TK = 512
SLC_TK = 512
MOE_ROWS = 256
COMBINE_ROWS = 256

F32 = jnp.float32
BF16 = jnp.bfloat16

_NT = (((1,), (1,)), ((), ()))


def _dot(a, b):
    return jnp.dot(a, b, preferred_element_type=F32)


def _dot_nt(a, b):
    return lax.dot_general(a, b, _NT, preferred_element_type=F32)


def _t5_bucket(rel):
    n = jnp.maximum(rel, 0)
    max_exact = REL_BUCKETS // 2
    nf = jnp.maximum(n, 1).astype(F32)
    large = max_exact + (jnp.log(nf / max_exact) / math.log(REL_MAX_DIST / max_exact)
                         * (REL_BUCKETS - max_exact)).astype(jnp.int32)
    large = jnp.minimum(large, REL_BUCKETS - 1)
    return jnp.where(n < max_exact, n, large)


def _layer_norm(z, g, b):
    mu = jnp.mean(z, -1, keepdims=True)
    zc = z - mu
    var = jnp.mean(zc * zc, -1, keepdims=True)
    return zc * lax.rsqrt(var + LN_EPS) * g + b


def _silu(x):
    return x * (1.0 / (1.0 + jnp.exp(-x)))


def _sigmoid(x):
    return 1.0 / (1.0 + jnp.exp(-x))


def _ada_kernel(c_ref, w_ref, b_ref, o_ref):
    cond = _silu(c_ref[...]).astype(BF16)
    o_ref[0] = _dot(cond, w_ref[0].astype(BF16)) + b_ref[0]


def _ada(c, w_ada, b_ada):
    depth, d, d6 = w_ada.shape
    bsz = c.shape[0]
    n_col = d6 // d
    return pl.pallas_call(
        _ada_kernel,
        grid=(depth, n_col),
        in_specs=[pl.BlockSpec((bsz, d), lambda l, j: (0, 0)),
                  pl.BlockSpec((1, d, d), lambda l, j: (l, 0, j)),
                  pl.BlockSpec((1, 1, d), lambda l, j: (l, 0, j))],
        out_specs=pl.BlockSpec((1, bsz, d), lambda l, j: (l * n_col + j, 0, 0)),
        out_shape=jax.ShapeDtypeStruct((depth * n_col, bsz, d), F32),
        name="ada",
    )(c, w_ada, b_ada.reshape(depth, 1, d6))


_PROJ_GROUPS = (("aq", 384, BF16), ("ak", 128, BF16), ("av", 128, BF16),
                ("bq", 256, BF16), ("bcmp", 128, BF16), ("bslc", 128, BF16), ("bwin", 128, BF16),
                ("cq", 384, BF16), ("ck", 384, BF16), ("cv", 384, BF16), ("misc", 128, F32))
_PROJ_WIDTH = sum(g[1] for g in _PROJ_GROUPS)


def _prep_w_in(w_in):
    scale = HEAD_DIM ** -0.5
    d = w_in.shape[0]
    hd = HEAD_DIM
    aq = w_in[:, 0:384].reshape(d, A_HEADS, hd)
    aq = aq[:, jnp.array([0, 3, 1, 4, 2, 5])].reshape(d, 384) * scale
    ak, av = w_in[:, 384:512], w_in[:, 512:640]
    bq = w_in[:, 640:896] * scale
    bkv = w_in[:, 896:1280]
    bg = w_in[:, 1280:1292]
    cq = w_in[:, 1292:1676] * scale
    ck, cv = w_in[:, 1676:2060], w_in[:, 2060:2444]
    cf = w_in[:, 2444:2450]
    misc = jnp.concatenate([bg, cf, jnp.zeros((d, 128 - 18), w_in.dtype)], axis=1)
    return jnp.concatenate([aq, ak, av, bq, bkv, cq, ck, cv, misc], axis=1).astype(BF16)


def _proj_kernel(x_ref, sc_ref, sh_ref, w_ref, *o_refs):
    h = (x_ref[0] * (1.0 + sc_ref[0]) + sh_ref[0]).astype(BF16)
    off = 0
    for o_ref, (_, width, dt) in zip(o_refs, _PROJ_GROUPS):
        o_ref[0] = _dot(h, w_ref[:, off:off + width]).astype(dt)
        off += width


def _in_proj(x, sc, sh, w):
    bsz, t, d = x.shape
    tm = PROJ_ROWS
    return pl.pallas_call(
        _proj_kernel,
        grid=(bsz, t // tm),
        in_specs=[pl.BlockSpec((1, tm, d), lambda b, i: (b, i, 0)),
                  pl.BlockSpec((1, 1, d), lambda b, i: (b, 0, 0)),
                  pl.BlockSpec((1, 1, d), lambda b, i: (b, 0, 0)),
                  pl.BlockSpec((d, _PROJ_WIDTH), lambda b, i: (0, 0))],
        out_specs=[pl.BlockSpec((1, tm, wd), lambda b, i: (b, i, 0)) for _, wd, _ in _PROJ_GROUPS],
        out_shape=[jax.ShapeDtypeStruct((bsz, t, wd), dt) for _, wd, dt in _PROJ_GROUPS],
        name="in_proj",
    )(x, sc, sh, w)


def _compress_kernel(x_ref, plo_ref, phi_ref, wlo_ref, whi_ref, o_ref):
    x = x_ref[0].astype(F32)
    p_lo = _dot((x + plo_ref[...]).astype(BF16), wlo_ref[...])
    p_hi = _dot((x + phi_ref[...]).astype(BF16), whi_ref[...])
    n = p_lo.shape[0]
    shifted = pltpu.roll(p_hi, n - 1, 0)
    row = lax.broadcasted_iota(jnp.int32, p_lo.shape, 0)
    o_ref[0] = jnp.where(row < n - 1, p_lo + shifted, 0.0).astype(BF16)


def _compress(bcmp, pos_k, w_k, pos_v, w_v):
    bsz, t, _ = bcmp.shape
    st, hd = NSA_CMP_STRIDE, HEAD_DIM
    n = t // st
    xr = bcmp.reshape(bsz, n, st * 2 * hd)
    pos = jnp.concatenate([pos_k, pos_v], axis=1)
    plo = pos[:st].reshape(1, st * 2 * hd)
    phi = pos[st:].reshape(1, st * 2 * hd)

    def big(w_kh, w_vh):
        z = jnp.zeros_like(w_kh)
        top = jnp.concatenate([w_kh, z], axis=2)
        bot = jnp.concatenate([z, w_vh], axis=2)
        return jnp.concatenate([top, bot], axis=1).reshape(st * 2 * hd, 2 * hd).astype(BF16)

    wk = w_k.reshape(NSA_CMP_LEN, hd, hd)
    wv = w_v.reshape(NSA_CMP_LEN, hd, hd)
    wlo = big(wk[:st], wv[:st])
    whi = big(wk[st:], wv[st:])
    kdim = st * 2 * hd
    return pl.pallas_call(
        _compress_kernel,
        grid=(bsz,),
        in_specs=[pl.BlockSpec((1, n, kdim), lambda b: (b, 0, 0)),
                  pl.BlockSpec((1, kdim), lambda b: (0, 0)),
                  pl.BlockSpec((1, kdim), lambda b: (0, 0)),
                  pl.BlockSpec((kdim, 2 * hd), lambda b: (0, 0)),
                  pl.BlockSpec((kdim, 2 * hd), lambda b: (0, 0))],
        out_specs=pl.BlockSpec((1, n, 2 * hd), lambda b: (b, 0, 0)),
        out_shape=jax.ShapeDtypeStruct((bsz, n, 2 * hd), BF16),
        name="nsa_compress",
    )(xr, plo, phi, wlo, whi)


def _swa_kernel(sink_ref, q_ref, kp_ref, kc_ref, vp_ref, vc_ref, bias_ref, o_ref):
    j = pl.program_id(1)
    q = q_ref[0]
    lane = lax.broadcasted_iota(jnp.int32, (BLOCK, LANES), 1)
    zero = jnp.zeros((BLOCK, LANES), BF16)
    pairs = [q[:, LANES * p:LANES * (p + 1)] for p in range(3)]
    qs = jnp.concatenate([jnp.where(lane < HEAD_DIM, qp, zero) for qp in pairs]
                         + [jnp.where(lane >= HEAD_DIM, qp, zero) for qp in pairs], axis=0)
    k2 = jnp.concatenate([kp_ref[0], kc_ref[0]], axis=0)
    v2 = jnp.concatenate([vp_ref[0], vc_ref[0]], axis=0)
    s = _dot_nt(qs, k2) + bias_ref[...]
    col = lax.broadcasted_iota(jnp.int32, (1, 2 * BLOCK), 1)
    s = s + jnp.where((col < BLOCK) & (j == 0), NEG, 0.0)
    ps = []
    for h in range(A_HEADS):
        sh = s[BLOCK * h:BLOCK * (h + 1)]
        sk = sink_ref[h]
        m = jnp.maximum(jnp.max(sh, axis=-1, keepdims=True), sk)
        e = jnp.exp(sh - m)
        den = jnp.sum(e, axis=-1, keepdims=True) + jnp.exp(sk - m)
        ps.append((e / den).astype(BF16))
    o = _dot(jnp.concatenate(ps, axis=0), v2)
    outs = [jnp.where(lane < HEAD_DIM, o[BLOCK * p:BLOCK * (p + 1)], o[BLOCK * (p + 3):BLOCK * (p + 4)])
            for p in range(3)]
    o_ref[0] = jnp.concatenate(outs, axis=1).astype(BF16)


def _swa(aq, ak, av, bias, sinks):
    bsz, t, _ = aq.shape
    nq = t // BLOCK
    prev = lambda b, j: (b, jnp.maximum(j - 1, 0), 0)
    cur = lambda b, j: (b, j, 0)
    return pl.pallas_call(
        _swa_kernel,
        grid=(bsz, nq),
        in_specs=[pl.BlockSpec(memory_space=pltpu.SMEM),
                  pl.BlockSpec((1, BLOCK, 384), cur),
                  pl.BlockSpec((1, BLOCK, LANES), prev),
                  pl.BlockSpec((1, BLOCK, LANES), cur),
                  pl.BlockSpec((1, BLOCK, LANES), prev),
                  pl.BlockSpec((1, BLOCK, LANES), cur),
                  pl.BlockSpec((A_HEADS * BLOCK, 2 * BLOCK), lambda b, j: (0, 0))],
        out_specs=pl.BlockSpec((1, BLOCK, 384), cur),
        out_shape=jax.ShapeDtypeStruct((bsz, t, 384), BF16),
        name="swa",
    )(sinks, aq, ak, ak, av, av, bias)


def _softmax_rows(s):
    m = jnp.max(s, axis=-1, keepdims=True)
    e = jnp.exp(s - m)
    return e / jnp.sum(e, axis=-1, keepdims=True)


def _nsa_kernel(q_ref, kvc_ref, kvs_ref, kvw_ref, gate_ref, biasc_ref, bnear_ref, bwin_ref, ovt_ref,
                o_ref, m_sc, l_sc, acc_sc):
    j = pl.program_id(1)
    nh, blk = B_HEADS, BLOCK
    rows = nh * blk
    lane = lax.broadcasted_iota(jnp.int32, (blk, LANES), 1)
    low = lane < HEAD_DIM

    qf = q_ref[0].astype(F32)
    parts = []
    for p in range(nh // 2):
        qp = qf[:, LANES * p:LANES * (p + 1)]
        parts.append(jnp.where(low, qp, 0.0))
        parts.append(jnp.where(low, pltpu.roll(qp, HEAD_DIM, 1), 0.0))
    qs = jnp.concatenate(parts, axis=0).astype(BF16)

    kvc = kvc_ref[0]
    s = _dot_nt(qs, kvc) + biasc_ref[0].astype(F32)
    p_cmp = _softmax_rows(s)
    trow = j * blk + lax.broadcasted_iota(jnp.int32, (rows, 1), 0) % blk
    p_cmp = p_cmp * (trow >= NSA_CMP_LEN - 1).astype(F32)
    o_cmp = _dot(p_cmp.astype(BF16), kvc)

    psum = p_cmp[0:blk]
    for h in range(1, nh):
        psum = psum + p_cmp[blk * h:blk * (h + 1)]
    ovt = ovt_ref[...]
    imp = jnp.zeros((ovt.shape[0], blk), F32)
    rem = psum
    for _ in range(3):
        piece = rem.astype(BF16)
        imp = imp + _dot_nt(ovt, piece)
        rem = rem - piece.astype(F32)
    n_slc = ovt.shape[0]
    jrow = lax.broadcasted_iota(jnp.int32, (n_slc, blk), 0)
    tl = lax.broadcasted_iota(jnp.int32, (n_slc, blk), 1)
    cur = (j * blk + tl) // NSA_SLC_LEN
    forced = (jrow == 0) | (jrow == cur) | (jrow == cur - 1)
    val = jnp.where(forced, FORCE_SCORE, jnp.where(jrow > cur, -1.0, imp))
    rank = jnp.zeros((n_slc, blk), F32)
    for i in range(n_slc):
        vi = val[i:i + 1, :]
        tie = jnp.where(jrow > i, 1.0, 0.0)
        rank = rank + jnp.where(vi > val, 1.0, jnp.where(vi == val, tie, 0.0))
    sel_t = jnp.where(rank < float(min(NSA_SLC_TOPK, n_slc)), 1.0, 0.0).astype(BF16)
    eye = jnp.where(lax.broadcasted_iota(jnp.int32, (blk, blk), 0)
                    == lax.broadcasted_iota(jnp.int32, (blk, blk), 1), 1.0, 0.0).astype(BF16)
    sel = _dot_nt(eye, sel_t).astype(BF16)

    def expand(base_blk, width):
        b = lax.broadcasted_iota(jnp.int32, (n_slc, width), 0)
        c = lax.broadcasted_iota(jnp.int32, (n_slc, width), 1)
        e = jnp.where(b == base_blk + c // NSA_SLC_LEN, 1.0, 0.0).astype(BF16)
        return _dot(sel, e)

    m_sc[...] = jnp.full(m_sc.shape, NEG, F32)
    l_sc[...] = jnp.zeros(l_sc.shape, F32)
    acc_sc[...] = jnp.zeros(acc_sc.shape, F32)

    def online(sb, kv):
        m_old = m_sc[...]
        m_new = jnp.maximum(m_old, jnp.max(sb, axis=-1, keepdims=True))
        alpha = jnp.exp(m_old - m_new)
        p = jnp.exp(sb - m_new)
        l_sc[...] = alpha * l_sc[...] + jnp.sum(p, axis=-1, keepdims=True)
        acc_sc[...] = alpha * acc_sc[...] + _dot(p.astype(BF16), kv)
        m_sc[...] = m_new

    far_keys = (j - 1) * blk

    def far_body(c, carry):
        start = pl.multiple_of(c * SLC_TK, SLC_TK)
        kv = kvs_ref[0, pl.ds(start, SLC_TK), :]
        sb = _dot_nt(qs, kv)
        msel = expand(c * (SLC_TK // NSA_SLC_LEN), SLC_TK)
        colk = start + lax.broadcasted_iota(jnp.int32, (1, SLC_TK), 1)
        madd = jnp.where(colk < far_keys, (msel - 1.0) * 1e30, NEG)
        sb = (sb.reshape(nh, blk, SLC_TK) + madd[None]).reshape(rows, SLC_TK)
        online(sb, kv)
        return carry

    n_far = jnp.maximum(far_keys + SLC_TK - 1, 0) // SLC_TK
    lax.fori_loop(0, n_far, far_body, 0)

    prev_start = pl.multiple_of(jnp.maximum(j - 1, 0) * blk, blk)
    cur_start = pl.multiple_of(j * blk, blk)
    kvn = jnp.concatenate([kvs_ref[0, pl.ds(prev_start, blk), :], kvs_ref[0, pl.ds(cur_start, blk), :]], axis=0)
    msel = expand((j - 1) * (blk // NSA_SLC_LEN), 2 * blk)
    sb = _dot_nt(qs, kvn) + bnear_ref[...]
    sb = (sb.reshape(nh, blk, 2 * blk) + ((msel - 1.0) * 1e30)[None]).reshape(rows, 2 * blk)
    online(sb, kvn)
    o_slc = acc_sc[...] / l_sc[...]

    n_w = NSA_WINDOW // blk + 1
    kvw = jnp.concatenate(
        [kvw_ref[0, pl.ds(pl.multiple_of(jnp.maximum(j - (n_w - 1) + i, 0) * blk, blk), blk), :]
         for i in range(n_w)], axis=0)
    colw = lax.broadcasted_iota(jnp.int32, (1, n_w * blk), 1)
    s = _dot_nt(qs, kvw) + bwin_ref[...] + jnp.where(colw // blk >= (n_w - 1) - j, 0.0, NEG)
    o_win = _dot(_softmax_rows(s).astype(BF16), kvw)

    g = _sigmoid(gate_ref[0])
    outs = []
    for h in range(nh):
        r0, r1 = blk * h, blk * (h + 1)
        outs.append(g[:, h:h + 1] * o_cmp[r0:r1] + g[:, nh + h:nh + h + 1] * o_slc[r0:r1]
                    + g[:, 2 * nh + h:2 * nh + h + 1] * o_win[r0:r1])
    pairs = [jnp.where(low, pltpu.roll(outs[2 * p], HEAD_DIM, 1), outs[2 * p + 1]) for p in range(nh // 2)]
    o_ref[0] = jnp.concatenate(pairs, axis=1).astype(BF16)


def _nsa(bq, kvc, bslc, bwin, misc, biasc, bnear, bwin_bias, ovt):
    bsz, t, _ = bq.shape
    nq = t // BLOCK
    n_cmp = kvc.shape[1]
    n_w = NSA_WINDOW // BLOCK + 1
    rows = B_HEADS * BLOCK
    cur = lambda b, j: (b, j, 0)
    full = lambda b, j: (b, 0, 0)
    return pl.pallas_call(
        _nsa_kernel,
        grid=(bsz, nq),
        in_specs=[pl.BlockSpec((1, BLOCK, 256), cur),
                  pl.BlockSpec((1, n_cmp, LANES), full),
                  pl.BlockSpec((1, t, LANES), full),
                  pl.BlockSpec((1, t, LANES), full),
                  pl.BlockSpec((1, BLOCK, LANES), cur),
                  pl.BlockSpec((1, rows, n_cmp), lambda b, j: (j, 0, 0)),
                  pl.BlockSpec((rows, 2 * BLOCK), lambda b, j: (0, 0)),
                  pl.BlockSpec((rows, n_w * BLOCK), lambda b, j: (0, 0)),
                  pl.BlockSpec(ovt.shape, lambda b, j: (0, 0))],
        out_specs=pl.BlockSpec((1, BLOCK, 256), cur),
        out_shape=jax.ShapeDtypeStruct((bsz, t, 256), BF16),
        scratch_shapes=[pltpu.VMEM((rows, 1), F32), pltpu.VMEM((rows, 1), F32),
                        pltpu.VMEM((rows, LANES), F32)],
        name="nsa",
    )(bq, kvc, bslc, bwin, misc, biasc, bnear, bwin_bias, ovt)


def _fox_kernel(q_ref, k_ref, v_ref, dq_ref, dk_ref, o_ref, m_sc, l_sc, acc_sc):
    j = pl.program_id(2)
    tq, tk = C_TQ, C_TK
    lane = lax.broadcasted_iota(jnp.int32, (tq, LANES), 1)
    qp = q_ref[0]
    zero = jnp.zeros_like(qp)
    qs = jnp.concatenate([jnp.where(lane < HEAD_DIM, qp, zero), jnp.where(lane >= HEAD_DIM, qp, zero)], axis=0)
    q_start = pl.multiple_of(j * tq, tq)
    d_base = jnp.max(dq_ref[0, 0, 0], axis=-1, keepdims=True)

    m_sc[...] = jnp.full(m_sc.shape, NEG, F32)
    l_sc[...] = jnp.zeros(l_sc.shape, F32)
    acc_sc[...] = jnp.zeros(acc_sc.shape, F32)

    def chunk(c, masked):
        start = pl.multiple_of(c * tk, tk)
        k = k_ref[0, pl.ds(start, tk), :]
        v = v_ref[0, pl.ds(start, tk), :]
        bias = d_base - dk_ref[0, 0, c]
        s = _dot_nt(qs, k).reshape(2, tq, tk) + bias[:, None, :]
        if masked:
            row = q_start + lax.broadcasted_iota(jnp.int32, (tq, tk), 0)
            col = start + lax.broadcasted_iota(jnp.int32, (tq, tk), 1)
            s = jnp.where((col <= row)[None], s, NEG)
        s = s.reshape(2 * tq, tk)
        m_old = m_sc[...]
        m_new = jnp.maximum(m_old, jnp.max(s, axis=-1, keepdims=True))
        alpha = jnp.exp(m_old - m_new)
        p = jnp.exp(s - m_new)
        l_sc[...] = alpha * l_sc[...] + jnp.sum(p, axis=-1, keepdims=True)
        acc_sc[...] = alpha * acc_sc[...] + _dot(p.astype(BF16), v)
        m_sc[...] = m_new

    n_full = (j * tq) // tk

    def body(c, carry):
        chunk(c, False)
        return carry

    lax.fori_loop(0, n_full, body, 0)
    chunk(n_full, True)
    o = acc_sc[...] / l_sc[...]
    o_ref[0] = jnp.where(lane < HEAD_DIM, o[0:tq], o[tq:2 * tq]).astype(BF16)


def _fox(cq, ck, cv, dcum):
    bsz, t, _ = cq.shape
    npair = C_HEADS // 2
    tq, tk = C_TQ, C_TK
    dq = dcum.reshape(bsz, npair, 2, t // tq, tq).transpose(0, 1, 3, 2, 4)
    dk = dcum.reshape(bsz, npair, 2, t // tk, tk).transpose(0, 1, 3, 2, 4)
    return pl.pallas_call(
        _fox_kernel,
        grid=(bsz, npair, t // tq),
        in_specs=[pl.BlockSpec((1, tq, LANES), lambda b, p, j: (b, j, p)),
                  pl.BlockSpec((1, t, LANES), lambda b, p, j: (b, 0, p)),
                  pl.BlockSpec((1, t, LANES), lambda b, p, j: (b, 0, p)),
                  pl.BlockSpec((1, 1, 1, 2, tq), lambda b, p, j: (b, p, j, 0, 0)),
                  pl.BlockSpec((1, 1, t // tk, 2, tk), lambda b, p, j: (b, p, 0, 0, 0))],
        out_specs=pl.BlockSpec((1, tq, LANES), lambda b, p, j: (b, j, p)),
        out_shape=jax.ShapeDtypeStruct((bsz, t, 384), BF16),
        scratch_shapes=[pltpu.VMEM((2 * tq, 1), F32), pltpu.VMEM((2 * tq, 1), F32),
                        pltpu.VMEM((2 * tq, LANES), F32)],
        name="fox",
    )(cq, ck, cv, dq, dk)


def _oproj_kernel(alpha, oa_ref, ob_ref, oc_ref, x_ref, g_ref, w_ref, lg_ref, lb_ref, o_ref):
    y = (_dot(oa_ref[0], w_ref[0:384, :]) + _dot(ob_ref[0], w_ref[384:640, :])
         + _dot(oc_ref[0], w_ref[640:1024, :]))
    z = alpha * x_ref[0] + (1.0 + g_ref[0]) * y
    o_ref[0] = _layer_norm(z, lg_ref[...], lb_ref[...])


def _out_proj(oa, ob, oc, x, gate, w_o, ln_g, ln_b, alpha):
    bsz, t, d = x.shape
    tm = PROJ_ROWS
    row = lambda b, i: (b, i, 0)
    return pl.pallas_call(
        functools.partial(_oproj_kernel, alpha),
        grid=(bsz, t // tm),
        in_specs=[pl.BlockSpec((1, tm, 384), row), pl.BlockSpec((1, tm, 256), row),
                  pl.BlockSpec((1, tm, 384), row), pl.BlockSpec((1, tm, d), row),
                  pl.BlockSpec((1, 1, d), lambda b, i: (b, 0, 0)),
                  pl.BlockSpec((d, d), lambda b, i: (0, 0)),
                  pl.BlockSpec((1, d), lambda b, i: (0, 0)),
                  pl.BlockSpec((1, d), lambda b, i: (0, 0))],
        out_specs=pl.BlockSpec((1, tm, d), row),
        out_shape=jax.ShapeDtypeStruct((bsz, t, d), F32),
        name="out_proj_ln",
    )(oa, ob, oc, x, gate, w_o, ln_g, ln_b)


def _moe_pre_kernel(x_ref, sc_ref, sh_ref, wr_ref, wgu_ref, wd_ref, h_ref, s_ref, y_ref):
    h = (x_ref[0] * (1.0 + sc_ref[0]) + sh_ref[0]).astype(BF16)
    h_ref[0] = h
    s_ref[0] = _sigmoid(_dot(h, wr_ref[...]))
    gu = _dot(h, wgu_ref[...])
    f = gu.shape[1] // 2
    hid = (_silu(gu[:, :f]) * gu[:, f:]).astype(BF16)
    y_ref[0] = _dot(hid, wd_ref[...]).astype(BF16)


def _moe_pre(x, sc, sh, w_router, w_gu, w_down):
    bsz, t, d = x.shape
    tm = PROJ_ROWS
    ne = w_router.shape[1]
    row = lambda b, i: (b, i, 0)
    const2 = lambda b, i: (0, 0)
    return pl.pallas_call(
        _moe_pre_kernel,
        grid=(bsz, t // tm),
        in_specs=[pl.BlockSpec((1, tm, d), row),
                  pl.BlockSpec((1, 1, d), lambda b, i: (b, 0, 0)),
                  pl.BlockSpec((1, 1, d), lambda b, i: (b, 0, 0)),
                  pl.BlockSpec(w_router.shape, const2),
                  pl.BlockSpec(w_gu.shape, const2),
                  pl.BlockSpec(w_down.shape, const2)],
        out_specs=[pl.BlockSpec((1, tm, d), row), pl.BlockSpec((1, tm, ne), row), pl.BlockSpec((1, tm, d), row)],
        out_shape=[jax.ShapeDtypeStruct((bsz, t, d), BF16), jax.ShapeDtypeStruct((bsz, t, ne), F32),
                   jax.ShapeDtypeStruct((bsz, t, d), BF16)],
        name="router_shared",
    )(x, sc, sh, w_router, w_gu, w_down)


def _expert_kernel(be_ref, nb_ref, x_ref, wgu_ref, wd_ref, o_ref):
    @pl.when(pl.program_id(0) < nb_ref[0])
    def _():
        gu = _dot(x_ref[...], wgu_ref[0])
        f = gu.shape[1] // 2
        hid = (_silu(gu[:, :f]) * gu[:, f:]).astype(BF16)
        o_ref[...] = _dot(hid, wd_ref[0]).astype(BF16)


def _experts(xs, block_e, n_used, w_gu, w_down):
    rows, d = xs.shape
    tm = MOE_ROWS
    n_blocks = rows // tm
    f2 = w_gu.shape[2]
    last = lambda i, nb: jnp.minimum(i, nb[0] - 1)
    grid_spec = pltpu.PrefetchScalarGridSpec(
        num_scalar_prefetch=2,
        grid=(n_blocks,),
        in_specs=[pl.BlockSpec((tm, d), lambda i, be, nb: (last(i, nb), 0)),
                  pl.BlockSpec((1, d, f2), lambda i, be, nb: (be[last(i, nb)], 0, 0)),
                  pl.BlockSpec((1, f2 // 2, d), lambda i, be, nb: (be[last(i, nb)], 0, 0))],
        out_specs=pl.BlockSpec((tm, d), lambda i, be, nb: (last(i, nb), 0)),
    )
    return pl.pallas_call(
        _expert_kernel,
        grid_spec=grid_spec,
        out_shape=jax.ShapeDtypeStruct((rows, d), BF16),
        name="experts",
    )(block_e, n_used, xs, w_gu, w_down)


def _combine_kernel(alpha, x_ref, ysh_ref, yg_ref, w_ref, g_ref, lg_ref, lb_ref, o_ref):
    y = ysh_ref[0].astype(F32)
    w = w_ref[0]
    for k in range(TOP_K):
        y = y + w[:, k:k + 1] * yg_ref[k, 0].astype(F32)
    z = alpha * x_ref[0] + (1.0 + g_ref[0]) * y
    o_ref[0] = _layer_norm(z, lg_ref[...], lb_ref[...])


def _combine(x, ysh, yg, gw, gate, ln_g, ln_b, alpha):
    bsz, t, d = x.shape
    tm = COMBINE_ROWS
    row = lambda b, i: (b, i, 0)
    return pl.pallas_call(
        functools.partial(_combine_kernel, alpha),
        grid=(bsz, t // tm),
        in_specs=[pl.BlockSpec((1, tm, d), row), pl.BlockSpec((1, tm, d), row),
                  pl.BlockSpec((TOP_K, 1, tm, d), lambda b, i: (0, b, i, 0)),
                  pl.BlockSpec((1, tm, TOP_K), row),
                  pl.BlockSpec((1, 1, d), lambda b, i: (b, 0, 0)),
                  pl.BlockSpec((1, d), lambda b, i: (0, 0)),
                  pl.BlockSpec((1, d), lambda b, i: (0, 0))],
        out_specs=pl.BlockSpec((1, tm, d), row),
        out_shape=jax.ShapeDtypeStruct((bsz, t, d), F32),
        name="combine_ln",
    )(x, ysh, yg, gw, gate, ln_g, ln_b)


def _bias_tiles(rel_bias, t):
    nq = t // BLOCK
    r = jnp.arange(BLOCK)[:, None]
    ta = rel_bias[:, :A_HEADS]
    tb = rel_bias[:, A_HEADS:]

    def tile(table, rel, valid):
        b = jnp.transpose(table[_t5_bucket(rel)], (2, 0, 1)).astype(F32)
        return jnp.where(valid[None], b, NEG)

    rel = BLOCK + r - jnp.arange(2 * BLOCK)[None, :]
    bias_a = tile(ta, rel, (rel >= 0) & (rel < A_WINDOW)).reshape(A_HEADS * BLOCK, 2 * BLOCK)
    far = tb[REL_BUCKETS - 1].astype(F32)[:, None, None]
    bnear = (tile(tb, rel, rel >= 0) - far).reshape(B_HEADS * BLOCK, 2 * BLOCK)
    n_w = NSA_WINDOW // BLOCK + 1
    relw = (n_w - 1) * BLOCK + r - jnp.arange(n_w * BLOCK)[None, :]
    bwin = tile(tb, relw, (relw >= 0) & (relw < NSA_WINDOW)).reshape(B_HEADS * BLOCK, n_w * BLOCK)
    n_c = t // NSA_CMP_STRIDE
    ci = jnp.arange(n_c)[None, :]
    relc = jnp.arange(t)[:, None] - (ci * NSA_CMP_STRIDE + NSA_CMP_LEN - 1)
    bc = tile(tb, relc, (relc >= 0) & (ci < n_c - 1))
    biasc = bc.reshape(B_HEADS, nq, BLOCK, n_c).transpose(1, 0, 2, 3).reshape(nq, B_HEADS * BLOCK, n_c)
    return bias_a, biasc.astype(BF16), bnear, bwin


def _overlap_t(t):
    n_c = t // NSA_CMP_STRIDE
    n_slc = t // NSA_SLC_LEN
    c_start = np.arange(n_c) * NSA_CMP_STRIDE
    s_start = np.arange(n_slc) * NSA_SLC_LEN
    ov = np.clip(np.minimum(c_start[:, None] + NSA_CMP_LEN, s_start[None, :] + NSA_SLC_LEN)
                 - np.maximum(c_start[:, None], s_start[None, :]), 0, None).astype(np.float32) / NSA_CMP_LEN
    ov[n_c - 1] = 0.0
    return jnp.asarray(ov.T, dtype=BF16)


def _route(scores, router_bias):
    choice = scores + router_bias.astype(F32)
    grouped = choice.reshape(-1, N_GROUPS, N_EXPERTS // N_GROUPS)
    gscore = jnp.sum(lax.top_k(grouped, 2)[0], axis=-1)
    _, gidx = lax.top_k(gscore, TOPK_GROUPS)
    gmask = jnp.sum(jax.nn.one_hot(gidx, N_GROUPS, dtype=F32), axis=-2) > 0
    masked = jnp.where(gmask[:, :, None], grouped, NEG).reshape(-1, N_EXPERTS)
    _, idx = lax.top_k(masked, TOP_K)
    w = jnp.take_along_axis(scores, idx, axis=-1)
    w = w / jnp.sum(w, -1, keepdims=True) * ROUTED_SCALE
    return idx, w


def _dispatch(idx):
    n = idx.shape[0]
    a = n * TOP_K
    tm = MOE_ROWS
    flat_e = idx.reshape(-1).astype(jnp.int32)
    order = jnp.argsort(flat_e)
    se = flat_e[order]
    counts = jnp.bincount(flat_e, length=N_EXPERTS)
    starts = jnp.cumsum(counts) - counts
    padded = (counts + tm - 1) // tm * tm
    pends = jnp.cumsum(padded)
    pstarts = pends - padded
    dest = (pstarts[se] + (jnp.arange(a) - starts[se])).astype(jnp.int32)
    n_blocks = -(-a // tm) + N_EXPERTS
    row_tok = jnp.zeros((n_blocks * tm,), jnp.int32).at[dest].set((order // TOP_K).astype(jnp.int32))
    block_e = jnp.clip(jnp.searchsorted(pends, jnp.arange(n_blocks) * tm, side='right'),
                       0, N_EXPERTS - 1).astype(jnp.int32)
    n_used = (pends[-1] // tm).astype(jnp.int32).reshape(1)
    pos = jnp.zeros((a,), jnp.int32).at[order].set(dest)
    return row_tok, block_e, n_used, pos.reshape(n, TOP_K)


def kernel(x, c, w_in, w_o, a_sinks, nsa_pos_k, nsa_w_k, nsa_pos_v, nsa_w_v, fox_f_bias, rel_bias,
           w_ada, b_ada, ln1_g, ln1_b, ln2_g, ln2_b, w_router, router_bias,
           w_exp_gate, w_exp_up, w_exp_down, w_sh_gate, w_sh_up, w_sh_down):
    depth = w_in.shape[0]
    alpha = (2.0 * depth) ** 0.25
    bsz, t, d = x.shape
    n = bsz * t

    ada = _ada(c, w_ada, b_ada)
    bias_a, biasc, bnear, bwin_bias = _bias_tiles(rel_bias, t)
    ovt = _overlap_t(t)
    perm = np.array([0, 3, 1, 4, 2, 5])

    for l in range(depth):
        sh1, sc1, g1, sh2, sc2, g2 = [ada[6 * l + i][:, None, :] for i in range(6)]
        (aq, ak, av, bq, bcmp, bslc, bwin, cq, ck, cv, misc) = _in_proj(x, sc1, sh1, _prep_w_in(w_in[l]))
        kvc = _compress(bcmp, nsa_pos_k[l], nsa_w_k[l], nsa_pos_v[l], nsa_w_v[l])
        o_a = _swa(aq, ak, av, bias_a, a_sinks[l].astype(F32))
        o_b = _nsa(bq, kvc, bslc, bwin, misc, biasc, bnear, bwin_bias, ovt)
        log_f = jax.nn.log_sigmoid(misc[:, :, 12:12 + C_HEADS] + fox_f_bias[l])
        dcum = jnp.transpose(jnp.cumsum(log_f, axis=1), (0, 2, 1)).reshape(bsz, C_HEADS // 2, 2, t)
        o_c = _fox(cq, ck, cv, dcum)
        wo = w_o[l]
        wo_a = wo[:384].reshape(A_HEADS, HEAD_DIM, d)[perm].reshape(384, d)
        wo_p = jnp.concatenate([wo_a, wo[384:]], axis=0).astype(BF16)
        x = _out_proj(o_a, o_b, o_c, x, g1, wo_p, ln1_g[l][None], ln1_b[l][None], alpha)
        w_gu_sh = jnp.concatenate([w_sh_gate[l], w_sh_up[l]], axis=1).astype(BF16)
        h, scores, ysh = _moe_pre(x, sc2, sh2, w_router[l].astype(BF16), w_gu_sh, w_sh_down[l].astype(BF16))
        idx, gw = _route(scores.reshape(n, N_EXPERTS), router_bias[l])
        row_tok, block_e, n_used, pos = _dispatch(idx)
        xs = jnp.take(h.reshape(n, d), row_tok, axis=0)
        w_gu = jnp.concatenate([w_exp_gate[l], w_exp_up[l]], axis=2).astype(BF16)
        ys = _experts(xs, block_e, n_used, w_gu, w_exp_down[l].astype(BF16))
        yg = jnp.take(ys, pos.T.reshape(-1), axis=0).reshape(TOP_K, bsz, t, d)
        x = _combine(x, ysh, yg, gw.reshape(bsz, t, TOP_K), g2, ln2_g[l][None], ln2_b[l][None], alpha)
    return x
```
